```python
import math
import jax, jax.numpy as jnp
from jax import lax
import numpy as np

D_MODEL = 1024
BATCH = 16
SEQ = 2048
DEPTH = 2

N_EVEN = (DEPTH + 1) // 2
N_ODD = DEPTH // 2
HEAD_DIM = 64
ATT_HEADS = 8
ATT_WIDTH = ATT_HEADS * HEAD_DIM
DILATED_BRANCHES = ((128, 1), (512, 4), (2048, 16))
ATT_QBLOCK = 64
ROPE_THETA = 10000.0
SGU_GROUPS = 8
SGU_DIM = 64
SGU_WIDTH = SGU_GROUPS * SGU_DIM
SGU_CHUNK = 128
IN_WIDTH = 3 * ATT_WIDTH + 2 * SGU_WIDTH
MIX_WIDTH = ATT_WIDTH + SGU_WIDTH
S5_GROUP = 16
S5_GROUPS = D_MODEL // S5_GROUP
S5_STATE = 64
PEER_HEADS = 8
PEER_QDIM = 256
PEER_HALF = PEER_QDIM // 2
PEER_NKEYS = 128
PEER_EXPERTS = PEER_NKEYS * PEER_NKEYS
PEER_TOPK = 16
PEER_TBLOCK = 128
RMS_EPS = 1e-6
LN_EPS = 1e-5

kernel_name = 'hybrid_dilated_sgu_s5_peer'


def rmsnorm(x, g):
    xf = x.astype(jnp.float32)
    y = xf * lax.rsqrt(jnp.mean(xf * xf, axis=-1, keepdims=True) + RMS_EPS)
    return (y * g.astype(jnp.float32)).astype(x.dtype)


def rotary(x, pos):
    half = HEAD_DIM // 2
    inv = jnp.power(ROPE_THETA, -2.0 * jnp.arange(half, dtype=jnp.float32) / HEAD_DIM)
    ang = pos[:, None] * inv[None, :]
    cos = jnp.cos(ang)[None, :, None, :]
    sin = jnp.sin(ang)[None, :, None, :]
    xf = x.astype(jnp.float32)
    x1, x2 = xf[..., :half], xf[..., half:]
    return jnp.concatenate([x1 * cos - x2 * sin, x2 * cos + x1 * sin], axis=-1).astype(x.dtype)


def dilated_attention(q, k, v):
    b, h, s, hd = q.shape
    n_blk = s // ATT_QBLOCK
    scale = HEAD_DIM ** -0.5

    def block(i):
        start = i * ATT_QBLOCK
        qb = lax.dynamic_slice_in_dim(q, start, ATT_QBLOCK, axis=2)
        t = start + jnp.arange(ATT_QBLOCK)
        outs, lses = [], []
        for window, dil in DILATED_BRANCHES:
            steps = jnp.arange(window // dil + 1)
            pos = t[:, None] - dil * steps[None, :]
            valid = pos >= 0
            pos = jnp.maximum(pos, 0)
            kg = k[:, :, pos]
            vg = v[:, :, pos]
            sc = jnp.einsum('bhqd,bhqld->bhql', qb, kg, preferred_element_type=jnp.float32) * scale
            sc = jnp.where(valid, sc, -jnp.inf)
            m = jnp.max(sc, axis=-1)
            p = jnp.exp(sc - m[..., None])
            den = jnp.sum(p, axis=-1)
            outs.append(jnp.einsum('bhql,bhqld->bhqd', p, vg.astype(jnp.float32)) / den[..., None])
            lses.append(m + jnp.log(den))
        w = jax.nn.softmax(jnp.stack(lses), axis=0)
        return jnp.einsum('rbhq,rbhqd->bhqd', w, jnp.stack(outs))

    out = lax.map(block, jnp.arange(n_blk))
    return out.transpose(1, 0, 3, 2, 4).reshape(b, s, h * hd)


def hybrid_ab(h, w_in, sgu_norm_g, sgu_w, sgu_b, w_out):
    b, s, _ = h.shape
    z = h @ w_in
    q, k, v, zb = jnp.split(z, [ATT_WIDTH, 2 * ATT_WIDTH, 3 * ATT_WIDTH], axis=-1)
    pos = jnp.arange(s, dtype=jnp.float32)
    q = rotary(q.reshape(b, s, ATT_HEADS, HEAD_DIM), pos).transpose(0, 2, 1, 3)
    k = rotary(k.reshape(b, s, ATT_HEADS, HEAD_DIM), pos).transpose(0, 2, 1, 3)
    v = v.reshape(b, s, ATT_HEADS, HEAD_DIM).transpose(0, 2, 1, 3)
    att = dilated_attention(q, k, v).astype(h.dtype)
    zb = jax.nn.gelu(zb)
    u, g = jnp.split(zb, 2, axis=-1)
    gf = g.reshape(b, s, SGU_GROUPS, SGU_DIM).astype(jnp.float32)
    mu = jnp.mean(gf, axis=-1, keepdims=True)
    var = jnp.mean(jnp.square(gf - mu), axis=-1, keepdims=True)
    gf = (gf - mu) * lax.rsqrt(var + LN_EPS) * sgu_norm_g.astype(jnp.float32)
    gc = gf.reshape(b, s // SGU_CHUNK, SGU_CHUNK, SGU_GROUPS, SGU_DIM)
    w_causal = jnp.tril(sgu_w.astype(jnp.float32))
    sp = jnp.einsum('hij,bcjhd->bcihd', w_causal, gc) + sgu_b.astype(jnp.float32).T[:, :, None]
    sgu = (u.astype(jnp.float32) * sp.reshape(b, s, SGU_WIDTH)).astype(h.dtype)
    return jnp.concatenate([att, sgu], axis=-1) @ w_out


def s5_block(h, lam_re, lam_im, b_re, b_im, c_re, c_im, d_skip, log_step, glu_w, glu_b):
    b, s, _ = h.shape
    hf = h.astype(jnp.float32)
    u = hf.reshape(b, s, S5_GROUPS, S5_GROUP)
    dt = jnp.exp(log_step.astype(jnp.float32))[:, None]
    lr = lam_re.astype(jnp.float32)
    li = lam_im.astype(jnp.float32)
    decay = jnp.exp(lr * dt)
    ar = decay * jnp.cos(li * dt)
    ai = decay * jnp.sin(li * dt)
    mag = lr * lr + li * li
    kr = ((ar - 1.0) * lr + ai * li) / mag
    ki = (ai * lr - (ar - 1.0) * li) / mag
    br = b_re.astype(jnp.float32)
    bi = b_im.astype(jnp.float32)
    bbr = kr[..., None] * br - ki[..., None] * bi
    bbi = kr[..., None] * bi + ki[..., None] * br
    xr = jnp.einsum('bsgc,gpc->bsgp', u, bbr)
    xi = jnp.einsum('bsgc,gpc->bsgp', u, bbi)
    a_r = jnp.broadcast_to(ar, (1, s) + ar.shape)
    a_i = jnp.broadcast_to(ai, (1, s) + ai.shape)

    def combine(e1, e2):
        a1r, a1i, b1r, b1i = e1
        a2r, a2i, b2r, b2i = e2
        return (a2r * a1r - a2i * a1i, a2r * a1i + a2i * a1r,
                a2r * b1r - a2i * b1i + b2r, a2r * b1i + a2i * b1r + b2i)

    _, _, sr, si = lax.associative_scan(combine, (a_r, a_i, xr, xi), axis=1)
    y = (jnp.einsum('bsgp,gcp->bsgc', sr, c_re.astype(jnp.float32))
         - jnp.einsum('bsgp,gcp->bsgc', si, c_im.astype(jnp.float32)))
    y = y.reshape(b, s, D_MODEL) + d_skip.astype(jnp.float32) * hf
    g = jax.nn.gelu(y)
    out = g * jax.nn.sigmoid(g @ glu_w.astype(jnp.float32) + glu_b.astype(jnp.float32))
    return out.astype(h.dtype)


def peer(h, wq, subkeys, u_tab, v_tab):
    b, s, d = h.shape
    xt = h.reshape(-1, d)
    t = xt.shape[0]
    q = (xt @ wq).reshape(t, PEER_HEADS, 2, PEER_HALF)
    sc = jnp.einsum('thpc,pkc->thpk', q, subkeys, preferred_element_type=jnp.float32)
    sv, si = lax.top_k(sc, PEER_TOPK)
    cand = sv[:, :, 0, :, None] + sv[:, :, 1, None, :]
    cidx = si[:, :, 0, :, None] * PEER_NKEYS + si[:, :, 1, None, :]
    cs, cpos = lax.top_k(cand.reshape(t, PEER_HEADS, -1), PEER_TOPK)
    idx = jnp.take_along_axis(cidx.reshape(t, PEER_HEADS, -1), cpos, axis=-1)
    gate = jax.nn.softmax(cs, axis=-1)
    nb = t // PEER_TBLOCK
    idx = idx.reshape(nb, PEER_TBLOCK, PEER_HEADS * PEER_TOPK)
    gate = gate.reshape(nb, PEER_TBLOCK, PEER_HEADS * PEER_TOPK)
    xb = xt.reshape(nb, PEER_TBLOCK, d)

    def expert_block(args):
        xbb, ib, gb = args
        act = jax.nn.gelu(jnp.einsum('td,tkd->tk', xbb, u_tab[ib], preferred_element_type=jnp.float32))
        return jnp.einsum('tk,tkd->td', gb * act, v_tab[ib].astype(jnp.float32))

    y = lax.map(expert_block, (xb, idx, gate))
    return y.reshape(b, s, d).astype(h.dtype)


def setup_inputs(seed: int = 0) -> dict:
    key = jax.random.key(seed)
    ks = iter(jax.random.split(key, 32))

    def nrm(shape, scale):
        return scale * jax.random.normal(next(ks), shape, jnp.float32)

    n = jnp.arange(S5_STATE, dtype=jnp.float32)
    return {
        'x': nrm((BATCH, SEQ, D_MODEL), 1.0),
        'norm_mix_g': 1.0 + nrm((DEPTH, D_MODEL), 0.02),
        'norm_ffn_g': 1.0 + nrm((DEPTH, D_MODEL), 0.02),
        'final_norm_g': 1.0 + nrm((D_MODEL,), 0.02),
        'mix_w_in': nrm((N_EVEN, D_MODEL, IN_WIDTH), D_MODEL ** -0.5),
        'sgu_norm_g': 1.0 + nrm((N_EVEN, SGU_GROUPS, SGU_DIM), 0.02),
        'sgu_w': nrm((N_EVEN, SGU_GROUPS, SGU_CHUNK, SGU_CHUNK), SGU_CHUNK ** -0.5),
        'sgu_b': 1.0 + nrm((N_EVEN, SGU_GROUPS, SGU_CHUNK), 0.02),
        'mix_w_out': nrm((N_EVEN, MIX_WIDTH, D_MODEL), MIX_WIDTH ** -0.5),
        's5_lambda_re': -0.5 + nrm((N_ODD, S5_GROUPS, S5_STATE), 0.01),
        's5_lambda_im': math.pi * n + nrm((N_ODD, S5_GROUPS, S5_STATE), 0.01),
        's5_b_re': nrm((N_ODD, S5_GROUPS, S5_STATE, S5_GROUP), (2 * S5_GROUP) ** -0.5),
        's5_b_im': nrm((N_ODD, S5_GROUPS, S5_STATE, S5_GROUP), (2 * S5_GROUP) ** -0.5),
        's5_c_re': nrm((N_ODD, S5_GROUPS, S5_GROUP, S5_STATE), S5_STATE ** -0.5),
        's5_c_im': nrm((N_ODD, S5_GROUPS, S5_GROUP, S5_STATE), S5_STATE ** -0.5),
        's5_d': nrm((N_ODD, D_MODEL), 1.0),
        's5_log_step': jax.random.uniform(next(ks), (N_ODD, S5_GROUPS), jnp.float32,
                                          minval=math.log(1e-3), maxval=math.log(1e-1)),
        'glu_w': nrm((N_ODD, D_MODEL, D_MODEL), D_MODEL ** -0.5),
        'glu_b': nrm((N_ODD, D_MODEL), 0.01),
        'peer_wq': nrm((DEPTH, D_MODEL, PEER_HEADS * PEER_QDIM), D_MODEL ** -0.5),
        'peer_subkeys': nrm((DEPTH, 2, PEER_NKEYS, PEER_HALF), PEER_HALF ** -0.5),
        'peer_u': nrm((DEPTH, PEER_EXPERTS, D_MODEL), D_MODEL ** -0.5),
        'peer_v': nrm((DEPTH, PEER_EXPERTS, D_MODEL), PEER_HEADS ** -0.5),
    }


def reference(x, norm_mix_g, norm_ffn_g, final_norm_g, mix_w_in, sgu_norm_g, sgu_w, sgu_b,
              mix_w_out, s5_lambda_re, s5_lambda_im, s5_b_re, s5_b_im, s5_c_re, s5_c_im,
              s5_d, s5_log_step, glu_w, glu_b, peer_wq, peer_subkeys, peer_u, peer_v):
    for layer in range(DEPTH):
        j = layer // 2
        h = rmsnorm(x, norm_mix_g[layer])
        if layer % 2 == 0:
            x = x + hybrid_ab(h, mix_w_in[j], sgu_norm_g[j], sgu_w[j], sgu_b[j], mix_w_out[j])
        else:
            x = x + s5_block(h, s5_lambda_re[j], s5_lambda_im[j], s5_b_re[j], s5_b_im[j],
                             s5_c_re[j], s5_c_im[j], s5_d[j], s5_log_step[j], glu_w[j], glu_b[j])
        h = rmsnorm(x, norm_ffn_g[layer])
        x = x + peer(h, peer_wq[layer], peer_subkeys[layer], peer_u[layer], peer_v[layer])
    return rmsnorm(x, final_norm_g)
```

```python
import functools
import math

import jax
import jax.numpy as jnp
from jax import lax
from jax.experimental import pallas as pl
from jax.experimental.pallas import tpu as pltpu

F32 = jnp.float32
BF16 = jnp.bfloat16
HIGHEST = lax.Precision.HIGHEST

D_MODEL = 1024
HEAD_DIM = 64
ATT_HEADS = 8
ATT_WIDTH = ATT_HEADS * HEAD_DIM
DILATED_BRANCHES = ((128, 1), (512, 4), (2048, 16))
ROPE_THETA = 10000.0
SGU_GROUPS = 8
SGU_DIM = 64
SGU_WIDTH = SGU_GROUPS * SGU_DIM
SGU_CHUNK = 128
IN_WIDTH = 3 * ATT_WIDTH + 2 * SGU_WIDTH
S5_GROUP = 16
S5_GROUPS = D_MODEL // S5_GROUP
S5_STATE = 64
PEER_HEADS = 8
PEER_QDIM = 256
PEER_HALF = PEER_QDIM // 2
PEER_NKEYS = 128
PEER_EXPERTS = PEER_NKEYS * PEER_NKEYS
PEER_TOPK = 16
RMS_EPS = 1e-6
LN_EPS = 1e-5

LANES = 128
VMEM_LIMIT = 56 * 1024 * 1024
ATT_BLOCK = 256
S5_CHUNK = 8
S5_LANEGROUPS = D_MODEL // LANES
S5_GPB = LANES // S5_GROUP
PEER_SLOTS = 4
NEG = -1e30


def _params(sem, vmem=VMEM_LIMIT):
    return pltpu.CompilerParams(dimension_semantics=sem, vmem_limit_bytes=vmem)


def _rms(x, g):
    return x * lax.rsqrt(jnp.mean(x * x, axis=-1, keepdims=True) + RMS_EPS) * g


def _norm_matmul_body(x_ref, g_ref, w_ref, o_ref):
    y = _rms(x_ref[...], g_ref[...])
    o_ref[...] = jnp.dot(y.astype(BF16), w_ref[...], preferred_element_type=F32)


def _norm_matmul(x2d, g, w, tm):
    t, d = x2d.shape
    n = w.shape[1]
    return pl.pallas_call(
        _norm_matmul_body,
        grid=(t // tm,),
        in_specs=[pl.BlockSpec((tm, d), lambda i: (i, 0)),
                  pl.BlockSpec((1, d), lambda i: (0, 0)),
                  pl.BlockSpec((d, n), lambda i: (0, 0))],
        out_specs=pl.BlockSpec((tm, n), lambda i: (i, 0)),
        out_shape=jax.ShapeDtypeStruct((t, n), F32),
        compiler_params=_params(("arbitrary",)),
        name="norm_matmul",
    )(x2d, g.reshape(1, d), w)


def _attn_body(q_ref, k_ref, v_ref, cos_ref, sin_ref, w_ref, o_ref, qs0, qs1, ks, vs0, vs1, *, qb):
    qi = pl.program_id(2)
    s_len = q_ref.shape[1]

    @pl.when(qi == 0)
    def _prep():
        lane = lax.broadcasted_iota(jnp.int32, (s_len, LANES), 1)
        first = (lane & (HEAD_DIM // 2)) == 0
        cos = cos_ref[...]
        sin = sin_ref[...]

        def rope(x):
            rot = jnp.where(first, -pltpu.roll(x, LANES - HEAD_DIM // 2, 1), pltpu.roll(x, HEAD_DIM // 2, 1))
            return x * cos + rot * sin

        lo = lane < HEAD_DIM
        qr = rope(q_ref[0]) * (HEAD_DIM ** -0.5)
        qs0[...] = jnp.where(lo, qr, 0.0).astype(BF16)
        qs1[...] = jnp.where(lo, 0.0, qr).astype(BF16)
        ks[...] = rope(k_ref[0]).astype(BF16)
        v = v_ref[0]
        vs0[...] = jnp.where(lo, v, 1.0).astype(BF16)
        vs1[...] = jnp.where(lo, 1.0, v).astype(BF16)

    q_rows = pl.ds(pl.multiple_of(qi * qb, qb), qb)
    q0 = qs0[q_rows, :]
    q1 = qs1[q_rows, :]
    lo = lax.broadcasted_iota(jnp.int32, (qb, LANES), 1) < HEAD_DIM

    def body(kj, carry):
        m0, a0, m1, a1 = carry
        k0 = pl.multiple_of(kj * qb, qb)
        kb = ks[pl.ds(k0, qb), :]
        w = w_ref[qi - kj].astype(F32)
        valid = w > 0.0

        def upd(qe, ve_ref, m, a):
            s = lax.dot_general(qe, kb, (((1,), (1,)), ((), ())), preferred_element_type=F32)
            s = jnp.where(valid, s, NEG)
            mn = jnp.maximum(m, jnp.max(s, axis=1, keepdims=True))
            p = jnp.exp(s - mn) * w
            a = jnp.exp(m - mn) * a + jnp.dot(p.astype(BF16), ve_ref[pl.ds(k0, qb), :],
                                              preferred_element_type=F32)
            return mn, a

        m0, a0 = upd(q0, vs0, m0, a0)
        m1, a1 = upd(q1, vs1, m1, a1)
        return m0, a0, m1, a1

    m_init = jnp.full((qb, 1), NEG, F32)
    a_init = jnp.zeros((qb, LANES), F32)
    _, a0, _, a1 = lax.fori_loop(0, qi + 1, body, (m_init, a_init, m_init, a_init))
    o0 = a0 / pltpu.roll(a0, HEAD_DIM, 1)
    o1 = a1 / pltpu.roll(a1, HEAD_DIM, 1)
    o_ref[0] = jnp.where(lo, o0, o1)


def _attn_tables(s, qb):
    half = HEAD_DIM // 2
    inv = jnp.power(ROPE_THETA, -2.0 * jnp.arange(half, dtype=F32) / HEAD_DIM)
    ang = jnp.arange(s, dtype=F32)[:, None] * inv[None, :]
    cos = jnp.tile(jnp.cos(ang), (1, LANES // half))
    sin = jnp.tile(jnp.sin(ang), (1, LANES // half))
    nblk = s // qb
    r = jnp.arange(qb)
    dist = (jnp.arange(nblk)[:, None, None] * qb + r[None, :, None] - r[None, None, :])
    mult = jnp.zeros(dist.shape, F32)
    for window, dil in DILATED_BRANCHES:
        mult = mult + ((dist >= 0) & (dist <= window) & (dist % dil == 0)).astype(F32)
    return cos, sin, mult.astype(BF16)


def _attention(z3, qb):
    b, s, _ = z3.shape
    nq = s // qb
    npair = ATT_WIDTH // LANES
    cos, sin, wtab = _attn_tables(s, qb)
    return pl.pallas_call(
        functools.partial(_attn_body, qb=qb),
        grid=(b, npair, nq),
        in_specs=[pl.BlockSpec((1, s, LANES), lambda i, j, k: (i, 0, j)),
                  pl.BlockSpec((1, s, LANES), lambda i, j, k: (i, 0, npair + j)),
                  pl.BlockSpec((1, s, LANES), lambda i, j, k: (i, 0, 2 * npair + j)),
                  pl.BlockSpec((s, LANES), lambda i, j, k: (0, 0)),
                  pl.BlockSpec((s, LANES), lambda i, j, k: (0, 0)),
                  pl.BlockSpec((nq, qb, qb), lambda i, j, k: (0, 0, 0))],
        out_specs=pl.BlockSpec((1, qb, LANES), lambda i, j, k: (i, k, j)),
        out_shape=jax.ShapeDtypeStruct((b, s, ATT_WIDTH), F32),
        scratch_shapes=[pltpu.VMEM((s, LANES), BF16)] * 5,
        compiler_params=_params(("arbitrary", "arbitrary", "arbitrary")),
        name="dilated_attention",
    )(z3, z3, z3, cos, sin, wtab)


def _sgu_body(zu_ref, zg_ref, avg_ref, gam_ref, wt_ref, bias_ref, o_ref):
    u = jax.nn.gelu(zu_ref[...])
    g = jax.nn.gelu(zg_ref[...])
    avg = avg_ref[...]
    mu = jnp.dot(g, avg, precision=HIGHEST, preferred_element_type=F32)
    dlt = g - mu
    var = jnp.dot(dlt * dlt, avg, precision=HIGHEST, preferred_element_type=F32)
    gn = dlt * lax.rsqrt(var + LN_EPS) * gam_ref[...]
    grp = lax.broadcasted_iota(jnp.int32, gn.shape, 1) // SGU_DIM
    sp = bias_ref[...]
    for h in range(SGU_GROUPS):
        gh = jnp.where(grp == h, gn, 0.0).astype(BF16)
        sp = sp + jnp.dot(wt_ref[h], gh, preferred_element_type=F32)
    o_ref[...] = u * sp


def _sgu(z2d, norm_g, w, bvec):
    t = z2d.shape[0]
    grp = jnp.arange(SGU_WIDTH) // SGU_DIM
    avg = (grp[:, None] == grp[None, :]).astype(F32) / SGU_DIM
    gam = norm_g.astype(F32).reshape(1, SGU_WIDTH)
    wt = jnp.tril(w.astype(F32)).astype(BF16)
    bias = jnp.repeat(bvec.astype(F32).T, SGU_DIM, axis=1)
    ucol = 3 * ATT_WIDTH // SGU_WIDTH
    return pl.pallas_call(
        _sgu_body,
        grid=(t // SGU_CHUNK,),
        in_specs=[pl.BlockSpec((SGU_CHUNK, SGU_WIDTH), lambda i: (i, ucol)),
                  pl.BlockSpec((SGU_CHUNK, SGU_WIDTH), lambda i: (i, ucol + 1)),
                  pl.BlockSpec((SGU_WIDTH, SGU_WIDTH), lambda i: (0, 0)),
                  pl.BlockSpec((1, SGU_WIDTH), lambda i: (0, 0)),
                  pl.BlockSpec((SGU_GROUPS, SGU_CHUNK, SGU_CHUNK), lambda i: (0, 0, 0)),
                  pl.BlockSpec((SGU_CHUNK, SGU_WIDTH), lambda i: (0, 0))],
        out_specs=pl.BlockSpec((SGU_CHUNK, SGU_WIDTH), lambda i: (i, 0)),
        out_shape=jax.ShapeDtypeStruct((t, SGU_WIDTH), F32),
        compiler_params=_params(("arbitrary",)),
        name="sgu",
    )(z2d, z2d, avg, gam, wt, bias)


def _outproj_body(att_ref, sgu_ref, wa_ref, wb_ref, x_ref, o_ref):
    o_ref[...] = (x_ref[...]
                  + jnp.dot(att_ref[...].astype(BF16), wa_ref[...], preferred_element_type=F32)
                  + jnp.dot(sgu_ref[...].astype(BF16), wb_ref[...], preferred_element_type=F32))


def _outproj(att2d, sgu2d, w_out, x2d, tm):
    t, d = x2d.shape
    wa = w_out[:ATT_WIDTH].astype(BF16)
    wb = w_out[ATT_WIDTH:].astype(BF16)
    return pl.pallas_call(
        _outproj_body,
        grid=(t // tm,),
        in_specs=[pl.BlockSpec((tm, ATT_WIDTH), lambda i: (i, 0)),
                  pl.BlockSpec((tm, SGU_WIDTH), lambda i: (i, 0)),
                  pl.BlockSpec((ATT_WIDTH, d), lambda i: (0, 0)),
                  pl.BlockSpec((SGU_WIDTH, d), lambda i: (0, 0)),
                  pl.BlockSpec((tm, d), lambda i: (i, 0))],
        out_specs=pl.BlockSpec((tm, d), lambda i: (i, 0)),
        out_shape=jax.ShapeDtypeStruct((t, d), F32),
        compiler_params=_params(("arbitrary",)),
        name="mixer_outproj",
    )(att2d, sgu2d, wa, wb, x2d)


def _staircase():
    pairs = [(a, b) for a in range(PEER_TOPK) for b in range(PEER_TOPK) if (a + 1) * (b + 1) <= PEER_TOPK]
    assert len(pairs) <= LANES
    return pairs


def _topk_body(q_ref, keys_ref, ea_ref, eb_ref, pad_ref, idx_ref, gate_ref):
    tb = q_ref.shape[0]
    lane = lax.broadcasted_iota(jnp.int32, (tb, LANES), 1)
    lane_f = lane.astype(F32)
    big = float(LANES * PEER_EXPERTS)

    def top16(s, payload):
        vals = jnp.zeros((tb, LANES), F32)
        pays = jnp.zeros((tb, LANES), F32)
        for j in range(PEER_TOPK):
            m = jnp.max(s, axis=1, keepdims=True)
            pm = jnp.min(jnp.where(s == m, payload, big), axis=1, keepdims=True)
            s = jnp.where(payload == pm, -jnp.inf, s)
            vals = jnp.where(lane == j, m, vals)
            pays = jnp.where(lane == j, pm, pays)
        return vals, pays

    ea = ea_ref[...]
    eb = eb_ref[...]
    idx_out = jnp.zeros((tb, LANES), jnp.int32)
    gate_out = jnp.zeros((tb, LANES), F32)
    for h in range(PEER_HEADS):
        sv, si = [], []
        for p in range(2):
            c0 = h * PEER_QDIM + p * PEER_HALF
            qs = q_ref[:, c0:c0 + PEER_HALF].astype(BF16)
            sc = lax.dot_general(qs, keys_ref[p], (((1,), (1,)), ((), ())), preferred_element_type=F32)
            v, i = top16(sc, lane_f)
            sv.append(v)
            si.append(i)
        cand = (jnp.dot(sv[0], ea, precision=HIGHEST, preferred_element_type=F32)
                + jnp.dot(sv[1], eb, precision=HIGHEST, preferred_element_type=F32) + pad_ref[...])
        cidx = (jnp.dot(si[0], ea, precision=HIGHEST, preferred_element_type=F32) * float(PEER_NKEYS)
                + jnp.dot(si[1], eb, precision=HIGHEST, preferred_element_type=F32))
        cs, pays = top16(cand, lane_f * float(PEER_EXPERTS) + cidx)
        head = lane < PEER_TOPK
        mx = jnp.max(jnp.where(head, cs, -jnp.inf), axis=1, keepdims=True)
        ex = jnp.where(head, jnp.exp(cs - mx), 0.0)
        gate = ex / jnp.sum(ex, axis=1, keepdims=True)
        eidx = pays.astype(jnp.int32) & (PEER_EXPERTS - 1)
        if h:
            gate = pltpu.roll(gate, h * PEER_TOPK, 1)
            eidx = pltpu.roll(eidx, h * PEER_TOPK, 1)
        here = (lane >= h * PEER_TOPK) & (lane < (h + 1) * PEER_TOPK)
        gate_out = jnp.where(here, gate, gate_out)
        idx_out = jnp.where(here, eidx, idx_out)
    idx_ref[...] = idx_out
    gate_ref[...] = gate_out


def _peer_route(q2d, subkeys, tb):
    t = q2d.shape[0]
    pairs = _staircase()
    ea = jnp.zeros((LANES, LANES), F32).at[jnp.array([a for a, _ in pairs]), jnp.arange(len(pairs))].set(1.0)
    eb = jnp.zeros((LANES, LANES), F32).at[jnp.array([b for _, b in pairs]), jnp.arange(len(pairs))].set(1.0)
    pad = jnp.where(jnp.arange(LANES) < len(pairs), 0.0, -jnp.inf).astype(F32).reshape(1, LANES)
    return pl.pallas_call(
        _topk_body,
        grid=(t // tb,),
        in_specs=[pl.BlockSpec((tb, PEER_HEADS * PEER_QDIM), lambda i: (i, 0)),
                  pl.BlockSpec((2, PEER_NKEYS, PEER_HALF), lambda i: (0, 0, 0)),
                  pl.BlockSpec((LANES, LANES), lambda i: (0, 0)),
                  pl.BlockSpec((LANES, LANES), lambda i: (0, 0)),
                  pl.BlockSpec((1, LANES), lambda i: (0, 0))],
        out_specs=[pl.BlockSpec((tb, LANES), lambda i: (i, 0)),
                   pl.BlockSpec((tb, LANES), lambda i: (i, 0))],
        out_shape=[jax.ShapeDtypeStruct((t, LANES), jnp.int32),
                   jax.ShapeDtypeStruct((t, LANES), F32)],
        compiler_params=_params(("arbitrary",)),
        name="peer_route",
    )(q2d, subkeys.astype(BF16), ea, eb, pad)


def _expert_body(idx_ref, x_ref, g_ref, gate_ref, fin_ref, uv_hbm, o_ref, h_scr, buf, sem, *, final_norm):
    tb, d = x_ref.shape
    npick = gate_ref.shape[1]
    h_scr[...] = _rms(x_ref[...], g_ref[...])

    def issue(t, slot):
        for k in range(npick):
            pltpu.make_async_copy(uv_hbm.at[pl.ds(idx_ref[t, k], 1)],
                                  buf.at[slot, pl.ds(k, 1)], sem.at[slot]).start()

    def wait(slot):
        pltpu.make_async_copy(uv_hbm.at[pl.ds(0, npick)], buf.at[slot], sem.at[slot]).wait()

    for s in range(PEER_SLOTS - 1):
        issue(s, s)

    eye = (lax.broadcasted_iota(jnp.int32, (npick, npick), 0)
           == lax.broadcasted_iota(jnp.int32, (npick, npick), 1))

    def body(t, carry):
        slot = lax.rem(t, PEER_SLOTS)
        nxt = t + PEER_SLOTS - 1

        @pl.when(nxt < tb)
        def _():
            issue(nxt, lax.rem(nxt, PEER_SLOTS))

        wait(slot)
        u = buf[slot, :, :d]
        v = buf[slot, :, d:]
        ht = h_scr[pl.ds(t, 1), :]
        act = jax.nn.gelu(jnp.sum(u * ht, axis=1, keepdims=True))
        gcol = jnp.sum(jnp.where(eye, gate_ref[pl.ds(t, 1), :], 0.0), axis=1, keepdims=True)
        y = jnp.sum((act * gcol) * v, axis=0, keepdims=True)
        o_ref[pl.ds(t, 1), :] = x_ref[pl.ds(t, 1), :] + y
        return carry

    lax.fori_loop(0, tb, body, 0)
    if final_norm:
        o_ref[...] = _rms(o_ref[...], fin_ref[...])


def _peer_experts(x2d, g, idx, gate, uv, final_g, tb):
    t, d = x2d.shape
    npick = gate.shape[1]
    assert tb >= PEER_SLOTS
    fin = (final_g if final_g is not None else jnp.ones((d,), F32)).reshape(1, d)
    return pl.pallas_call(
        functools.partial(_expert_body, final_norm=final_g is not None),
        grid=(t // tb,),
        in_specs=[pl.BlockSpec((tb, npick), lambda i: (i, 0), memory_space=pltpu.SMEM),
                  pl.BlockSpec((tb, d), lambda i: (i, 0)),
                  pl.BlockSpec((1, d), lambda i: (0, 0)),
                  pl.BlockSpec((tb, npick), lambda i: (i, 0)),
                  pl.BlockSpec((1, d), lambda i: (0, 0)),
                  pl.BlockSpec(memory_space=pl.ANY)],
        out_specs=pl.BlockSpec((tb, d), lambda i: (i, 0)),
        out_shape=jax.ShapeDtypeStruct((t, d), F32),
        scratch_shapes=[pltpu.VMEM((tb, d), F32),
                        pltpu.VMEM((PEER_SLOTS, npick, 2 * d), F32),
                        pltpu.SemaphoreType.DMA((PEER_SLOTS,))],
        compiler_params=_params(("arbitrary",)),
        name="peer_experts",
    )(idx, x2d, g.reshape(1, d), gate, fin, uv)


def _peer_layer(x2d, g, wq, subkeys, u_tab, v_tab, final_g, tm, tb_route, tb_exp):
    q = _norm_matmul(x2d, g, wq.astype(BF16), tm)
    idx, gate = _peer_route(q, subkeys, tb_route)
    uv = jnp.concatenate([u_tab, v_tab], axis=1)
    return _peer_experts(x2d, g, idx, gate, uv, final_g, tb_exp)


def _s5_operators(lam_re, lam_im, b_re, b_im, c_re, c_im, log_step):
    L, G, P, C, GPB = S5_CHUNK, S5_GROUPS, S5_STATE, S5_GROUP, S5_GPB
    NB = G // GPB
    dt = jnp.exp(log_step.astype(F32))[:, None]
    lr = lam_re.astype(F32)
    li = lam_im.astype(F32)
    decay = jnp.exp(lr * dt)
    ar = decay * jnp.cos(li * dt)
    ai = decay * jnp.sin(li * dt)
    mag = lr * lr + li * li
    kr = ((ar - 1.0) * lr + ai * li) / mag
    ki = (ai * lr - (ar - 1.0) * li) / mag
    br = b_re.astype(F32)
    bi = b_im.astype(F32)
    bbr = kr[..., None] * br - ki[..., None] * bi
    bbi = kr[..., None] * bi + ki[..., None] * br
    cr = c_re.astype(F32)
    ci = c_im.astype(F32)
    n = jnp.arange(L + 1, dtype=F32)[:, None, None]
    pwr = jnp.exp(lr * dt * n) * jnp.cos(li * dt * n)
    pwi = jnp.exp(lr * dt * n) * jnp.sin(li * dt * n)
    car = cr[None] * pwr[:, :, None, :] - ci[None] * pwi[:, :, None, :]
    cai = cr[None] * pwi[:, :, None, :] + ci[None] * pwr[:, :, None, :]
    eye = jnp.eye(GPB, dtype=F32)
    kern = (jnp.einsum('ngop,gpc->gnoc', car[:L], bbr, precision=HIGHEST)
            - jnp.einsum('ngop,gpc->gnoc', cai[:L], bbi, precision=HIGHEST))
    tj = jnp.arange(L)[:, None]
    ti = jnp.arange(L)[None, :]
    toep = jnp.where((ti >= tj)[None, :, :, None, None], kern[:, jnp.clip(ti - tj, 0, L - 1)], 0.0)
    toep = toep.reshape(NB, GPB, L, L, C, C)
    m = (jnp.transpose(toep, (0, 2, 1, 5, 3, 4))[:, :, :, :, :, None, :]
         * eye[None, None, :, None, None, :, None]).reshape(NB, L * LANES, L * LANES)
    rev = L - 1 - jnp.arange(L)
    injr = pwr[rev][:, :, :, None] * bbr[None] - pwi[rev][:, :, :, None] * bbi[None]
    inji = pwr[rev][:, :, :, None] * bbi[None] + pwi[rev][:, :, :, None] * bbr[None]
    inj = jnp.stack([injr, inji], axis=0).reshape(2, L, NB, GPB, P, C)
    inj = jnp.transpose(inj, (2, 1, 3, 5, 0, 4))
    bm = (inj[:, :, :, :, :, None, :] * eye[None, None, :, None, None, :, None]).reshape(NB, L * LANES, 2 * GPB * P)
    out = jnp.stack([car[1:], -cai[1:]], axis=0).reshape(2, L, NB, GPB, C, P)
    out = jnp.transpose(out, (2, 0, 3, 5, 1, 4))
    cm = (out[:, :, :, :, :, None, :] * eye[None, None, :, None, None, :, None]).reshape(NB, 2 * GPB * P, L * LANES)
    a8 = jnp.stack([pwr[L].reshape(NB, GPB * P), pwi[L].reshape(NB, GPB * P)], axis=1)
    return m.astype(BF16), bm.astype(BF16), cm.astype(BF16), a8


def _s5_body(x_ref, g_ref, m_ref, bm_ref, cm_ref, a8_ref, d_ref, w_ref, b_ref, o_ref,
             hn, yg, sloc, sprev, carry):
    hf = pl.program_id(1)
    gb = pl.program_id(2)
    L, NB = S5_CHUNK, S5_LANEGROUPS
    rc = x_ref.shape[0]
    half = sloc.shape[1] // 2

    @pl.when(gb == 0)
    def _normalise():
        for t in range(L):
            n = _rms(x_ref[:, t * D_MODEL:(t + 1) * D_MODEL], g_ref[...])
            for j in range(NB):
                hn[j, :, t * LANES:(t + 1) * LANES] = n[:, j * LANES:(j + 1) * LANES]

    @pl.when(hf == 0)
    def _reset():
        carry[gb] = jnp.zeros((1, 2 * half), F32)

    xg = hn[gb].astype(BF16)
    y_intra = jnp.dot(xg, m_ref[0], preferred_element_type=F32)
    sloc[...] = jnp.dot(xg, bm_ref[0], preferred_element_type=F32)
    ar = a8_ref[0, 0:1, :]
    ai = a8_ref[0, 1:2, :]
    c0 = carry[gb]

    def scan(r, c):
        cr, ci = c
        sprev[pl.ds(r, 1), :half] = cr
        sprev[pl.ds(r, 1), half:] = ci
        row = sloc[pl.ds(r, 1), :]
        return (ar * cr - ai * ci + row[:, :half], ar * ci + ai * cr + row[:, half:])

    cr, ci = lax.fori_loop(0, rc, scan, (c0[:, :half], c0[:, half:]))
    carry[gb] = jnp.concatenate([cr, ci], axis=1)
    yg[gb] = y_intra + jnp.dot(sprev[...].astype(BF16), cm_ref[0], preferred_element_type=F32)

    @pl.when(gb == NB - 1)
    def _glu():
        for t in range(L):
            y = jnp.concatenate([yg[j, :, t * LANES:(t + 1) * LANES] for j in range(NB)], axis=1)
            h = jnp.concatenate([hn[j, :, t * LANES:(t + 1) * LANES] for j in range(NB)], axis=1)
            g = jax.nn.gelu(y + d_ref[...] * h)
            gatev = jax.nn.sigmoid(jnp.dot(g.astype(BF16), w_ref[...], preferred_element_type=F32) + b_ref[...])
            o_ref[:, t * D_MODEL:(t + 1) * D_MODEL] = x_ref[:, t * D_MODEL:(t + 1) * D_MODEL] + g * gatev


def _s5_layer(x3, g, lam_re, lam_im, b_re, b_im, c_re, c_im, d_skip, log_step, glu_w, glu_b, rc):
    b, s, d = x3.shape
    L, NB = S5_CHUNK, S5_LANEGROUPS
    m, bm, cm, a8 = _s5_operators(lam_re, lam_im, b_re, b_im, c_re, c_im, log_step)
    ncb = s // L
    nh = ncb // rc
    x2 = x3.reshape(b * ncb, L * d)
    nstate = 2 * S5_GPB * S5_STATE
    out = pl.pallas_call(
        _s5_body,
        grid=(b, nh, NB),
        in_specs=[pl.BlockSpec((rc, L * d), lambda i, j, k: (i * nh + j, 0)),
                  pl.BlockSpec((1, d), lambda i, j, k: (0, 0)),
                  pl.BlockSpec((1, L * LANES, L * LANES), lambda i, j, k: (k, 0, 0)),
                  pl.BlockSpec((1, L * LANES, nstate), lambda i, j, k: (k, 0, 0)),
                  pl.BlockSpec((1, nstate, L * LANES), lambda i, j, k: (k, 0, 0)),
                  pl.BlockSpec((1, 2, nstate // 2), lambda i, j, k: (k, 0, 0)),
                  pl.BlockSpec((1, d), lambda i, j, k: (0, 0)),
                  pl.BlockSpec((d, d), lambda i, j, k: (0, 0)),
                  pl.BlockSpec((1, d), lambda i, j, k: (0, 0))],
        out_specs=pl.BlockSpec((rc, L * d), lambda i, j, k: (i * nh + j, 0)),
        out_shape=jax.ShapeDtypeStruct((b * ncb, L * d), F32),
        scratch_shapes=[pltpu.VMEM((NB, rc, L * LANES), F32),
                        pltpu.VMEM((NB, rc, L * LANES), F32),
                        pltpu.VMEM((rc, nstate), F32),
                        pltpu.VMEM((rc, nstate), F32),
                        pltpu.VMEM((NB, 1, nstate), F32)],
        compiler_params=_params(("arbitrary", "arbitrary", "arbitrary")),
        name="s5_glu",
    )(x2, g.reshape(1, d), m, bm, cm, a8, d_skip.astype(F32).reshape(1, d),
      glu_w.astype(BF16), glu_b.astype(F32).reshape(1, d))
    return out.reshape(b, s, d)


def _forward(x, norm_mix_g, norm_ffn_g, final_norm_g, mix_w_in, sgu_norm_g, sgu_w, sgu_b, mix_w_out,
             s5_lambda_re, s5_lambda_im, s5_b_re, s5_b_im, s5_c_re, s5_c_im, s5_d, s5_log_step,
             glu_w, glu_b, peer_wq, peer_subkeys, peer_u, peer_v, *, tm, qb, tb_route, tb_exp, s5_rows):
    b, s, d = x.shape
    t = b * s
    x2d = x.reshape(t, d)
    z = _norm_matmul(x2d, norm_mix_g[0], mix_w_in[0].astype(BF16), tm)
    att = _attention(z.reshape(b, s, IN_WIDTH), qb)
    sgu = _sgu(z, sgu_norm_g[0], sgu_w[0], sgu_b[0])
    x2d = _outproj(att.reshape(t, ATT_WIDTH), sgu, mix_w_out[0], x2d, tm)
    x2d = _peer_layer(x2d, norm_ffn_g[0], peer_wq[0], peer_subkeys[0], peer_u[0], peer_v[0], None,
                      tm, tb_route, tb_exp)
    x3 = _s5_layer(x2d.reshape(b, s, d), norm_mix_g[1], s5_lambda_re[0], s5_lambda_im[0], s5_b_re[0], s5_b_im[0],
                   s5_c_re[0], s5_c_im[0], s5_d[0], s5_log_step[0], glu_w[0], glu_b[0], s5_rows)
    x2d = _peer_layer(x3.reshape(t, d), norm_ffn_g[1], peer_wq[1], peer_subkeys[1], peer_u[1], peer_v[1],
                      final_norm_g, tm, tb_route, tb_exp)
    return x2d.reshape(b, s, d)


def kernel(x, norm_mix_g, norm_ffn_g, final_norm_g, mix_w_in, sgu_norm_g, sgu_w, sgu_b, mix_w_out, s5_lambda_re, s5_lambda_im, s5_b_re, s5_b_im, s5_c_re, s5_c_im, s5_d, s5_log_step, glu_w, glu_b, peer_wq, peer_subkeys, peer_u, peer_v):
    s = x.shape[1]
    return _forward(x, norm_mix_g, norm_ffn_g, final_norm_g, mix_w_in, sgu_norm_g, sgu_w, sgu_b, mix_w_out,
                    s5_lambda_re, s5_lambda_im, s5_b_re, s5_b_im, s5_c_re, s5_c_im, s5_d, s5_log_step,
                    glu_w, glu_b, peer_wq, peer_subkeys, peer_u, peer_v,
                    tm=256, qb=min(ATT_BLOCK, s), tb_route=128, tb_exp=64,
                    s5_rows=min(128, s // S5_CHUNK))
```
